```python
import jax, jax.numpy as jnp
from jax import lax
import numpy as np

D_MODEL = 1024
BATCH = 8
SEQ = 8192
DEPTH = 1

D_MIX = D_MODEL
D_A = D_MIX // 2
D_B = D_MIX - D_A
HEAD_DIM = 64
N_GROUPS_A = D_A // HEAD_DIM
N_GROUPS_B = D_B // HEAD_DIM
K_SHORT = 3
K_CONFORMER = 31
K_FFN = 3
D_FF = 2816
D_IN = 3 * D_A + 2 * D_B
RMS_EPS = 1e-6
LN_EPS = 1e-5

kernel_name = "hybrid_shortconv_conformer_convffn_block"


def rmsnorm(x, g):
    xf = x.astype(jnp.float32)
    y = xf * lax.rsqrt(jnp.mean(xf * xf, axis=-1, keepdims=True) + RMS_EPS)
    return (y * g.astype(jnp.float32)).astype(x.dtype)


def layernorm(x, g, b):
    xf = x.astype(jnp.float32)
    mu = jnp.mean(xf, axis=-1, keepdims=True)
    var = jnp.mean(jnp.square(xf - mu), axis=-1, keepdims=True)
    y = (xf - mu) * lax.rsqrt(var + LN_EPS)
    return (y * g.astype(jnp.float32) + b.astype(jnp.float32)).astype(x.dtype)


def dwconv(x, w):
    k, c = w.shape
    rhs = w[:, None, :].astype(x.dtype)
    return lax.conv_general_dilated(
        x, rhs, window_strides=(1,), padding=[(k // 2, k // 2)],
        dimension_numbers=("NWC", "WIO", "NWC"), feature_group_count=c)


def setup_inputs(seed: int = 0) -> dict:
    key = jax.random.key(seed)
    ks = jax.random.split(key, 16)
    f32 = jnp.float32

    def nrm(k, shape, scale):
        return jax.random.normal(k, shape, f32) * scale

    return {
        "x": jax.random.normal(ks[0], (BATCH, SEQ, D_MODEL), f32),
        "norm_mix_g": 1.0 + nrm(ks[1], (DEPTH, D_MODEL), 0.05),
        "w_in": nrm(ks[2], (DEPTH, D_MODEL, D_IN), D_MODEL ** -0.5),
        "conv_a_w": nrm(ks[3], (DEPTH, K_SHORT, D_A), K_SHORT ** -0.5),
        "conv_b_w": nrm(ks[4], (DEPTH, K_CONFORMER, D_B), K_CONFORMER ** -0.5),
        "conv_b_b": nrm(ks[5], (DEPTH, D_B), 0.02),
        "ln_b_g": 1.0 + nrm(ks[6], (DEPTH, D_B), 0.05),
        "ln_b_b": nrm(ks[7], (DEPTH, D_B), 0.02),
        "w_out": nrm(ks[8], (DEPTH, D_MIX, D_MODEL), D_MIX ** -0.5),
        "norm_ffn_g": 1.0 + nrm(ks[9], (DEPTH, D_MODEL), 0.05),
        "w_gate": nrm(ks[10], (DEPTH, D_MODEL, D_FF), D_MODEL ** -0.5),
        "w_up": nrm(ks[11], (DEPTH, D_MODEL, D_FF), D_MODEL ** -0.5),
        "conv_ffn_w": nrm(ks[12], (DEPTH, K_FFN, D_FF), K_FFN ** -0.5),
        "w_down": nrm(ks[13], (DEPTH, D_FF, D_MODEL), D_FF ** -0.5),
        "norm_final_g": 1.0 + nrm(ks[14], (D_MODEL,), 0.05),
    }


def reference(x, norm_mix_g, w_in, conv_a_w, conv_b_w, conv_b_b, ln_b_g, ln_b_b,
              w_out, norm_ffn_g, w_gate, w_up, conv_ffn_w, w_down, norm_final_g):
    for l in range(DEPTH):
        h = rmsnorm(x, norm_mix_g[l])
        z = jnp.einsum("bsd,de->bse", h, w_in[l])
        a_h, a_bg, a_cg, b_val, b_gate = jnp.split(
            z, [D_A, 2 * D_A, 3 * D_A, 3 * D_A + D_B], axis=-1)
        y_a = a_bg * dwconv(a_cg * a_h, conv_a_w[l])
        u = b_val * jax.nn.sigmoid(b_gate)
        u = dwconv(u, conv_b_w[l]) + conv_b_b[l].astype(u.dtype)
        y_b = jax.nn.silu(layernorm(u, ln_b_g[l], ln_b_b[l]))
        y = jnp.concatenate([y_a, y_b], axis=-1)
        x = x + jnp.einsum("bse,ed->bsd", y, w_out[l])

        h = rmsnorm(x, norm_ffn_g[l])
        g = dwconv(jnp.einsum("bsd,df->bsf", h, w_gate[l]), conv_ffn_w[l])
        v = jnp.einsum("bsd,df->bsf", h, w_up[l])
        x = x + jnp.einsum("bsf,fd->bsd", jax.nn.silu(g) * v, w_down[l])

    return rmsnorm(x, norm_final_g)
```

```python
import functools

import jax
import jax.numpy as jnp
from jax import lax
from jax.experimental import pallas as pl
from jax.experimental.pallas import tpu as pltpu

RMS_EPS = 1e-6
LN_EPS = 1e-5

V7X_LANES = 128
V7X_F32_SUBLANES = 8
V7X_BF16_SUBLANES = 16
V7X_VMEM_BYTES = 64 * 1024 * 1024

SEQ_TILE = 512
MIX_HALO = V7X_BF16_SUBLANES
FFN_HALO = V7X_F32_SUBLANES
FFN_CHUNK = 256
CONV_ROWS = 128
VMEM_LIMIT_BYTES = V7X_VMEM_BYTES - 8 * 1024 * 1024


def _rms_scale(x):
    return x * lax.rsqrt(jnp.mean(x * x, axis=-1, keepdims=True) + RMS_EPS)


def _shift_rows(v, offset):
    n = v.shape[0]
    return pltpu.roll(v, (-offset) % n, axis=0)


def _mixer_kernel(xp_ref, xm_ref, xn_ref, g_ref, win_ref, caw_ref, cbw_ref, cbb_ref,
                  lng_ref, lnb_ref, wout_ref, o_ref, h_scr, u_scr, y_scr):
    t = xm_ref.shape[1]
    d_a = caw_ref.shape[1]
    d_b = cbw_ref.shape[1]
    k_b = cbw_ref.shape[0]
    i = pl.program_id(1)
    last = pl.num_programs(1) - 1
    g = g_ref[...]

    xp = jnp.where(i > 0, xp_ref[0], 0.0)
    xn = jnp.where(i < last, xn_ref[0], 0.0)
    h_scr[0:MIX_HALO, :] = (_rms_scale(xp) * g).astype(jnp.bfloat16)
    h_scr[MIX_HALO:MIX_HALO + t, :] = (_rms_scale(xm_ref[0]) * g).astype(jnp.bfloat16)
    h_scr[MIX_HALO + t:, :] = (_rms_scale(xn) * g).astype(jnp.bfloat16)
    h = h_scr[...]

    za = jnp.dot(h, win_ref[:, 0:3 * d_a], preferred_element_type=jnp.float32)
    a_h, a_bg, a_cg = za[:, 0:d_a], za[:, d_a:2 * d_a], za[:, 2 * d_a:3 * d_a]
    p = a_cg * a_h
    conv_a = (caw_ref[0:1, :] * _shift_rows(p, -1) + caw_ref[1:2, :] * p
              + caw_ref[2:3, :] * _shift_rows(p, 1))
    y_a = a_bg[MIX_HALO:MIX_HALO + t] * conv_a[MIX_HALO:MIX_HALO + t]
    y_scr[:, 0:d_a] = y_a.astype(jnp.bfloat16)

    zb = jnp.dot(h, win_ref[:, 3 * d_a:], preferred_element_type=jnp.float32)
    u_scr[...] = zb[:, 0:d_b] * jax.nn.sigmoid(zb[:, d_b:])

    lead = MIX_HALO - k_b // 2
    sub = V7X_F32_SUBLANES
    rows = CONV_ROWS
    for c0 in range(0, d_b, V7X_LANES):
        cs = slice(c0, c0 + V7X_LANES)
        for r0 in range(0, t, rows):
            acc = None
            for b in range(sub):
                vb = None
                for a in range((k_b + lead + sub - 1) // sub):
                    j = sub * a + b - lead
                    if j < 0 or j >= k_b:
                        continue
                    term = cbw_ref[j:j + 1, cs] * u_scr[r0 + sub * a:r0 + sub * a + rows + sub, cs]
                    vb = term if vb is None else vb + term
                if vb is None:
                    continue
                part = (vb if b == 0 else _shift_rows(vb, b))[0:rows]
                acc = part if acc is None else acc + part
            u_conv = acc + cbb_ref[:, cs]
            u_scr[r0:r0 + rows, cs] = u_conv

    uc = u_scr[0:t, :]
    mu = jnp.mean(uc, axis=-1, keepdims=True)
    cen = uc - mu
    var = jnp.mean(cen * cen, axis=-1, keepdims=True)
    ln = cen * lax.rsqrt(var + LN_EPS) * lng_ref[...] + lnb_ref[...]
    y_scr[:, d_a:] = (ln * jax.nn.sigmoid(ln)).astype(jnp.bfloat16)

    mix = jnp.dot(y_scr[...], wout_ref[...], preferred_element_type=jnp.float32)
    o_ref[0] = xm_ref[0] + mix


def _ffn_kernel(xp_ref, xm_ref, xn_ref, g_ref, wg_ref, wu_ref, cw_ref, wd_ref, gf_ref,
                o_ref, x_scr, act_scr):
    t = xm_ref.shape[1]
    d_ff = wg_ref.shape[1]
    i = pl.program_id(1)
    last = pl.num_programs(1) - 1
    g = g_ref[...]

    x_scr[0:FFN_HALO, :] = jnp.where(i > 0, xp_ref[0], 0.0)
    x_scr[FFN_HALO:FFN_HALO + t, :] = xm_ref[0]
    x_scr[FFN_HALO + t:, :] = jnp.where(i < last, xn_ref[0], 0.0)
    h_all = (_rms_scale(x_scr[...]) * g).astype(jnp.bfloat16)
    h_main = (_rms_scale(xm_ref[0]) * g).astype(jnp.bfloat16)

    for f0 in range(0, d_ff, FFN_CHUNK):
        fs = slice(f0, f0 + FFN_CHUNK)
        gp = jnp.dot(h_all, wg_ref[:, fs], preferred_element_type=jnp.float32)
        v = jnp.dot(h_main, wu_ref[:, fs], preferred_element_type=jnp.float32)
        conv = (cw_ref[0:1, fs] * _shift_rows(gp, -1) + cw_ref[1:2, fs] * gp
                + cw_ref[2:3, fs] * _shift_rows(gp, 1))[FFN_HALO:FFN_HALO + t]
        act_scr[:, fs] = (conv * jax.nn.sigmoid(conv) * v).astype(jnp.bfloat16)

    y = jnp.dot(act_scr[...], wd_ref[...], preferred_element_type=jnp.float32)
    o_ref[0] = _rms_scale(xm_ref[0] + y) * gf_ref[...]


def _const_spec(shape):
    return pl.BlockSpec(shape, lambda b, i: (0,) * len(shape), pipeline_mode=pl.Buffered(1))


def _row_specs(t, halo, s, d):
    per_tile = t // halo
    last_block = s // halo - 1
    prev = pl.BlockSpec((1, halo, d), lambda b, i: (b, jnp.maximum(i * per_tile - 1, 0), 0))
    main = pl.BlockSpec((1, t, d), lambda b, i: (b, i, 0))
    nxt = pl.BlockSpec((1, halo, d), lambda b, i: (b, jnp.minimum((i + 1) * per_tile, last_block), 0))
    return [prev, main, nxt]


_COMPILER_PARAMS = pltpu.CompilerParams(
    dimension_semantics=("arbitrary", "arbitrary"), vmem_limit_bytes=VMEM_LIMIT_BYTES)


def _mixer(x, g, w_in, conv_a_w, conv_b_w, conv_b_b, ln_g, ln_b, w_out):
    bsz, s, d = x.shape
    t = SEQ_TILE
    d_a, d_b = conv_a_w.shape[1], conv_b_w.shape[1]
    assert s % t == 0 and t % CONV_ROWS == 0 and conv_b_w.shape[0] // 2 <= MIX_HALO
    params = [g, w_in, conv_a_w, conv_b_w, conv_b_b, ln_g, ln_b, w_out]
    return pl.pallas_call(
        _mixer_kernel,
        grid=(bsz, s // t),
        in_specs=_row_specs(t, MIX_HALO, s, d) + [_const_spec(p.shape) for p in params],
        out_specs=pl.BlockSpec((1, t, d), lambda b, i: (b, i, 0)),
        out_shape=jax.ShapeDtypeStruct(x.shape, x.dtype),
        scratch_shapes=[
            pltpu.VMEM((t + 2 * MIX_HALO, d), jnp.bfloat16),
            pltpu.VMEM((t + 2 * MIX_HALO, d_b), jnp.float32),
            pltpu.VMEM((t, d_a + d_b), jnp.bfloat16),
        ],
        compiler_params=_COMPILER_PARAMS,
        name="mixer",
    )(x, x, x, *params)


def _ffn(x, g, w_gate, w_up, conv_w, w_down, g_final):
    bsz, s, d = x.shape
    t = SEQ_TILE
    d_ff = w_gate.shape[1]
    assert s % t == 0 and d_ff % FFN_CHUNK == 0
    params = [g, w_gate, w_up, conv_w, w_down, g_final]
    return pl.pallas_call(
        _ffn_kernel,
        grid=(bsz, s // t),
        in_specs=_row_specs(t, FFN_HALO, s, d) + [_const_spec(p.shape) for p in params],
        out_specs=pl.BlockSpec((1, t, d), lambda b, i: (b, i, 0)),
        out_shape=jax.ShapeDtypeStruct(x.shape, x.dtype),
        scratch_shapes=[
            pltpu.VMEM((t + 2 * FFN_HALO, d), jnp.float32),
            pltpu.VMEM((t, d_ff), jnp.bfloat16),
        ],
        compiler_params=_COMPILER_PARAMS,
        name="ffn",
    )(x, x, x, *params)


def kernel(x, norm_mix_g, w_in, conv_a_w, conv_b_w, conv_b_b, ln_b_g, ln_b_b, w_out,
           norm_ffn_g, w_gate, w_up, conv_ffn_w, w_down, norm_final_g):
    depth = w_in.shape[0]
    assert depth == 1, "the final norm is fused into the last layer's ffn call"
    bf16 = jnp.bfloat16
    row = lambda v: v.reshape(1, -1)
    x = _mixer(x, row(norm_mix_g[0]), w_in[0].astype(bf16), conv_a_w[0], conv_b_w[0],
               row(conv_b_b[0]), row(ln_b_g[0]), row(ln_b_b[0]), w_out[0].astype(bf16))
    return _ffn(x, row(norm_ffn_g[0]), w_gate[0].astype(bf16), w_up[0].astype(bf16),
                conv_ffn_w[0], w_down[0].astype(bf16), row(norm_final_g))
```

```python
import functools

import jax
import jax.numpy as jnp
from jax import lax
from jax.experimental import pallas as pl
from jax.experimental.pallas import tpu as pltpu

RMS_EPS = 1e-6
LN_EPS = 1e-5

V7X_LANES = 128
V7X_F32_SUBLANES = 8
V7X_BF16_SUBLANES = 16
V7X_VMEM_BYTES = 64 * 1024 * 1024

SEQ_TILE = 512
FFN_HALO = V7X_F32_SUBLANES
X_HALO = 2 * V7X_BF16_SUBLANES
A_TRIM = V7X_BF16_SUBLANES
FFN_CHUNK = 256
CONV_ROWS = 88
CONV_PIECES_PER_CHUNK = 3
PIPELINE_STAGES = 3
VMEM_LIMIT_BYTES = V7X_VMEM_BYTES - 4 * 1024 * 1024


def _rms_scale(x):
    return x * lax.rsqrt(jnp.mean(x * x, axis=-1, keepdims=True) + RMS_EPS)


def _to_lane_tiles(tiles_ref, v):
    for c in range(v.shape[1] // V7X_LANES):
        tiles_ref[c, :, :] = v[:, c * V7X_LANES:(c + 1) * V7X_LANES]


def _dwconv_tile(tiles_ref, c, w_ref, cs, first_row, rows):
    acc = None
    for j in range(w_ref.shape[0]):
        term = w_ref[j:j + 1, cs] * tiles_ref[c, first_row + j:first_row + j + rows, :]
        acc = term if acc is None else acc + term
    return acc


def _block_kernel(tiles_per_seq,
                  xp_ref, xm_ref, xn_ref, gm_ref, win_ref, caw_ref, cbw_ref, cbb_ref,
                  lng_ref, lnb_ref, wout_ref, gf_ref, wg_ref, wu_ref, cfw_ref, wd_ref, gz_ref,
                  o_ref,
                  x_scr, h_scr, hf_scr, hm_scr, p_scr, u_scr, c_scr, ya_scr, yb_scr, gp_scr, act_scr):
    t = xm_ref.shape[1]
    d_a = caw_ref.shape[1]
    d_b = cbw_ref.shape[1]
    d_ff = wg_ref.shape[1]
    tm = t + 2 * FFN_HALO
    s = pl.program_id(0)
    n_tiles = pl.num_programs(0) - (PIPELINE_STAGES - 1)
    i1 = lax.rem(jnp.minimum(s, n_tiles - 1), tiles_per_seq)
    i2 = lax.rem(jnp.clip(s - 1, 0, n_tiles - 1), tiles_per_seq)
    first1, last1 = i1 == 0, i1 == tiles_per_seq - 1
    first2, last2 = i2 == 0, i2 == tiles_per_seq - 1
    wr = lax.rem(s, 2)
    rd = 1 - wr
    x1 = lax.rem(s, PIPELINE_STAGES)
    x2 = lax.rem(s + 2, PIPELINE_STAGES)
    x3 = lax.rem(s + 1, PIPELINE_STAGES)
    lead_f = FFN_HALO - cfw_ref.shape[0] // 2
    n_chunks = d_ff // FFN_CHUNK

    @pl.when(s == 0)
    def _():
        for ref in (x_scr, hf_scr, hm_scr, c_scr, ya_scr):
            ref[...] = jnp.zeros(ref.shape, ref.dtype)

    def up_chunk(n):
        f0 = n * FFN_CHUNK
        fs = slice(f0, f0 + FFN_CHUNK)
        gp_tiles = gp_scr.at[n % 2]
        _to_lane_tiles(gp_tiles, jnp.dot(hf_scr[rd], wg_ref[:, fs], preferred_element_type=jnp.float32))
        v = jnp.dot(hm_scr[rd], wu_ref[:, fs], preferred_element_type=jnp.float32)
        for c in range(FFN_CHUNK // V7X_LANES):
            cs = slice(f0 + c * V7X_LANES, f0 + (c + 1) * V7X_LANES)
            conv = _dwconv_tile(gp_tiles, c, cfw_ref, cs, lead_f, t)
            vc = v[:, c * V7X_LANES:(c + 1) * V7X_LANES]
            act_scr[:, cs] = (conv * jax.nn.sigmoid(conv) * vc).astype(jnp.bfloat16)

    uc = c_scr[rd]
    mu = jnp.mean(uc, axis=-1, keepdims=True)
    cen = uc - mu
    var = jnp.mean(cen * cen, axis=-1, keepdims=True)
    ln = cen * lax.rsqrt(var + LN_EPS) * lng_ref[...] + lnb_ref[...]
    yb_scr[...] = (ln * jax.nn.sigmoid(ln)).astype(jnp.bfloat16)

    up_chunk(0)

    gm = gm_ref[...]
    xp = jnp.where(first1, 0.0, xp_ref[0])
    xn = jnp.where(last1, 0.0, xn_ref[0])
    h_scr[0:X_HALO, :] = (_rms_scale(xp) * gm).astype(jnp.bfloat16)
    h_scr[X_HALO:X_HALO + t, :] = (_rms_scale(xm_ref[0]) * gm).astype(jnp.bfloat16)
    h_scr[X_HALO + t:, :] = (_rms_scale(xn) * gm).astype(jnp.bfloat16)
    out0 = X_HALO - FFN_HALO
    x_new = x_scr.at[x1]
    x_new[0:FFN_HALO, :] = xp[out0:, :]
    x_new[FFN_HALO:FFN_HALO + t, :] = xm_ref[0]
    x_new[FFN_HALO + t:, :] = xn[0:FFN_HALO, :]

    h_b = h_scr[...]
    b_val = jnp.dot(h_b, win_ref[:, 3 * d_a:3 * d_a + d_b], preferred_element_type=jnp.float32)
    b_gate = jnp.dot(h_b, win_ref[:, 3 * d_a + d_b:], preferred_element_type=jnp.float32)
    _to_lane_tiles(u_scr, b_val * jax.nn.sigmoid(b_gate))

    up_chunk(1)

    h_a = h_scr[A_TRIM:t + 2 * X_HALO - A_TRIM, :]
    a_h = jnp.dot(h_a, win_ref[:, 0:d_a], preferred_element_type=jnp.float32)
    a_cg = jnp.dot(h_a, win_ref[:, 2 * d_a:3 * d_a], preferred_element_type=jnp.float32)
    _to_lane_tiles(p_scr, a_cg * a_h)
    a_bg = jnp.dot(h_a, win_ref[:, d_a:2 * d_a], preferred_element_type=jnp.float32)
    lead_a = out0 - A_TRIM - caw_ref.shape[0] // 2
    y_a = []
    for c in range(d_a // V7X_LANES):
        cs = slice(c * V7X_LANES, (c + 1) * V7X_LANES)
        conv_a = _dwconv_tile(p_scr, c, caw_ref, cs, lead_a, tm)
        y_a.append((a_bg[out0 - A_TRIM:out0 - A_TRIM + tm, cs] * conv_a).astype(jnp.bfloat16))
    ya_scr[wr] = jnp.concatenate(y_a, axis=1)

    lead_b = out0 - cbw_ref.shape[0] // 2
    c_new = c_scr.at[wr]
    pieces = [(c, r0) for c in range(d_b // V7X_LANES) for r0 in range(0, tm, CONV_ROWS)]
    for n in range(2, n_chunks):
        up_chunk(n)
        k = CONV_PIECES_PER_CHUNK * (n - 2)
        for c, r0 in pieces[k:k + CONV_PIECES_PER_CHUNK]:
            cs = slice(c * V7X_LANES, (c + 1) * V7X_LANES)
            conv_b = _dwconv_tile(u_scr, c, cbw_ref, cs, r0 + lead_b, CONV_ROWS)
            c_new[r0:r0 + CONV_ROWS, cs] = conv_b + cbb_ref[:, cs]

    mix = (jnp.dot(ya_scr[rd], wout_ref[0:d_a, :], preferred_element_type=jnp.float32)
           + jnp.dot(yb_scr[...], wout_ref[d_a:, :], preferred_element_type=jnp.float32))
    xmid = x_scr.at[x2]
    xmid[0:FFN_HALO, :] = jnp.where(first2, 0.0, xmid[0:FFN_HALO, :] + mix[0:FFN_HALO])
    xmid[FFN_HALO:FFN_HALO + t, :] = xmid[FFN_HALO:FFN_HALO + t, :] + mix[FFN_HALO:FFN_HALO + t]
    xmid[FFN_HALO + t:, :] = jnp.where(last2, 0.0, xmid[FFN_HALO + t:, :] + mix[FFN_HALO + t:])
    hn = _rms_scale(xmid[...]) * gf_ref[...]
    hf_scr[wr] = hn.astype(jnp.bfloat16)
    hm_scr[wr] = hn[FFN_HALO:FFN_HALO + t].astype(jnp.bfloat16)

    half = t // 2
    for r0 in (0, half):
        y = jnp.dot(act_scr[r0:r0 + half, :], wd_ref[...], preferred_element_type=jnp.float32)
        x_res = x_scr[x3, FFN_HALO + r0:FFN_HALO + r0 + half, :]
        o_ref[0, r0:r0 + half, :] = _rms_scale(x_res + y) * gz_ref[...]


def _const_spec(shape):
    return pl.BlockSpec(shape, lambda s: (0,) * len(shape), pipeline_mode=pl.Buffered(1))


def _block(x, gm, w_in, conv_a_w, conv_b_w, conv_b_b, ln_g, ln_b, w_out,
           gf, w_gate, w_up, conv_f_w, w_down, gz):
    bsz, seq, d = x.shape
    t = SEQ_TILE
    d_a, d_b, d_ff = conv_a_w.shape[1], conv_b_w.shape[1], w_gate.shape[1]
    tm = t + 2 * FFN_HALO
    n_chunks = d_ff // FFN_CHUNK
    assert seq % t == 0 and t % X_HALO == 0 and tm % CONV_ROWS == 0 and d_ff % FFN_CHUNK == 0
    assert (d_b // V7X_LANES) * (tm // CONV_ROWS) <= CONV_PIECES_PER_CHUNK * (n_chunks - 2)
    assert conv_b_w.shape[0] // 2 + FFN_HALO <= X_HALO and conv_a_w.shape[0] // 2 + FFN_HALO <= X_HALO - A_TRIM
    tiles_per_seq = seq // t
    n_tiles = bsz * tiles_per_seq
    halo_per_tile = t // X_HALO
    last_halo_block = seq // X_HALO - 1

    def in_tile(s):
        tile = jnp.minimum(s, n_tiles - 1)
        return tile // tiles_per_seq, tile % tiles_per_seq

    def prev_map(s):
        b, i = in_tile(s)
        return b, jnp.maximum(i * halo_per_tile - 1, 0), 0

    def main_map(s):
        b, i = in_tile(s)
        return b, i, 0

    def next_map(s):
        b, i = in_tile(s)
        return b, jnp.minimum((i + 1) * halo_per_tile, last_halo_block), 0

    def out_map(s):
        tile = jnp.maximum(s - (PIPELINE_STAGES - 1), 0)
        return tile // tiles_per_seq, tile % tiles_per_seq, 0

    params = [gm, w_in, conv_a_w, conv_b_w, conv_b_b, ln_g, ln_b, w_out, gf, w_gate, w_up, conv_f_w, w_down, gz]
    f32, bf16 = jnp.float32, jnp.bfloat16
    return pl.pallas_call(
        functools.partial(_block_kernel, tiles_per_seq),
        grid=(n_tiles + PIPELINE_STAGES - 1,),
        in_specs=[pl.BlockSpec((1, X_HALO, d), prev_map), pl.BlockSpec((1, t, d), main_map),
                  pl.BlockSpec((1, X_HALO, d), next_map)] + [_const_spec(p.shape) for p in params],
        out_specs=pl.BlockSpec((1, t, d), out_map),
        out_shape=jax.ShapeDtypeStruct(x.shape, x.dtype),
        scratch_shapes=[
            pltpu.VMEM((PIPELINE_STAGES, tm, d), f32),
            pltpu.VMEM((t + 2 * X_HALO, d), bf16),
            pltpu.VMEM((2, tm, d), bf16),
            pltpu.VMEM((2, t, d), bf16),
            pltpu.VMEM((d_a // V7X_LANES, t + 2 * (X_HALO - A_TRIM), V7X_LANES), f32),
            pltpu.VMEM((d_b // V7X_LANES, t + 2 * X_HALO, V7X_LANES), f32),
            pltpu.VMEM((2, tm, d_b), f32),
            pltpu.VMEM((2, tm, d_a), bf16),
            pltpu.VMEM((tm, d_b), bf16),
            pltpu.VMEM((2, FFN_CHUNK // V7X_LANES, tm, V7X_LANES), f32),
            pltpu.VMEM((t, d_ff), bf16),
        ],
        compiler_params=pltpu.CompilerParams(
            dimension_semantics=("arbitrary",), vmem_limit_bytes=VMEM_LIMIT_BYTES),
        name="block",
    )(x, x, x, *params)


def kernel(x, norm_mix_g, w_in, conv_a_w, conv_b_w, conv_b_b, ln_b_g, ln_b_b, w_out,
           norm_ffn_g, w_gate, w_up, conv_ffn_w, w_down, norm_final_g):
    depth = w_in.shape[0]
    assert depth == 1, "the final norm is fused into the last layer's call"
    bf16 = jnp.bfloat16
    row = lambda v: v.reshape(1, -1)
    return _block(x, row(norm_mix_g[0]), w_in[0].astype(bf16), conv_a_w[0], conv_b_w[0],
                  row(conv_b_b[0]), row(ln_b_g[0]), row(ln_b_b[0]), w_out[0].astype(bf16),
                  row(norm_ffn_g[0]), w_gate[0].astype(bf16), w_up[0].astype(bf16),
                  conv_ffn_w[0], w_down[0].astype(bf16), row(norm_final_g))
```

```python
import functools

import jax
import jax.numpy as jnp
from jax import lax
from jax.experimental import pallas as pl
from jax.experimental.pallas import tpu as pltpu

RMS_EPS = 1e-6
LN_EPS = 1e-5

V7X_LANES = 128
V7X_F32_SUBLANES = 8
V7X_BF16_SUBLANES = 16
V7X_VMEM_BYTES = 64 * 1024 * 1024

SEQ_TILE = 512
FFN_HALO = V7X_F32_SUBLANES
X_HALO = 2 * V7X_BF16_SUBLANES
A_TRIM = V7X_BF16_SUBLANES
FFN_CHUNK = 256
CONV_ROWS = 88
CONV_PIECES_PER_CHUNK = 3
W_OUT_BEFORE_LAST_CHUNKS = 2
PIPELINE_STAGES = 3
VMEM_LIMIT_BYTES = V7X_VMEM_BYTES - 4 * 1024 * 1024


def _rms_scale(x):
    return x * lax.rsqrt(jnp.mean(x * x, axis=-1, keepdims=True) + RMS_EPS)


def _to_lane_tiles(tiles_ref, v):
    for c in range(v.shape[1] // V7X_LANES):
        tiles_ref[c, :, :] = v[:, c * V7X_LANES:(c + 1) * V7X_LANES]


def _dwconv_tile(tiles_ref, c, w_ref, cs, first_row, rows):
    acc = None
    for j in range(w_ref.shape[0]):
        term = w_ref[j:j + 1, cs] * tiles_ref[c, first_row + j:first_row + j + rows, :]
        acc = term if acc is None else acc + term
    return acc


def _block_kernel(tiles_per_seq,
                  xp_ref, xm_ref, xn_ref, gm_ref, win_ref, caw_ref, cbw_ref, cbb_ref,
                  lng_ref, lnb_ref, wout_ref, gf_ref, wg_ref, wu_ref, cfw_ref, wd_ref, gz_ref,
                  o_ref,
                  x_scr, h_scr, hf_scr, hm_scr, p_scr, u_scr, c_scr, ya_scr, yb_scr, gp_scr, act_scr):
    t = xm_ref.shape[1]
    d_a = caw_ref.shape[1]
    d_b = cbw_ref.shape[1]
    d_ff = wg_ref.shape[1]
    tm = t + 2 * FFN_HALO
    s = pl.program_id(0)
    n_tiles = pl.num_programs(0) - (PIPELINE_STAGES - 1)
    i1 = lax.rem(jnp.minimum(s, n_tiles - 1), tiles_per_seq)
    i2 = lax.rem(jnp.clip(s - 1, 0, n_tiles - 1), tiles_per_seq)
    first1, last1 = i1 == 0, i1 == tiles_per_seq - 1
    first2, last2 = i2 == 0, i2 == tiles_per_seq - 1
    wr = lax.rem(s, 2)
    rd = 1 - wr
    x1 = lax.rem(s, PIPELINE_STAGES)
    x2 = lax.rem(s + 2, PIPELINE_STAGES)
    x3 = lax.rem(s + 1, PIPELINE_STAGES)
    lead_f = FFN_HALO - cfw_ref.shape[0] // 2
    n_chunks = d_ff // FFN_CHUNK

    @pl.when(s == 0)
    def _():
        for ref in (x_scr, hf_scr, hm_scr, c_scr, ya_scr):
            ref[...] = jnp.zeros(ref.shape, ref.dtype)

    def up_chunk(n):
        f0 = n * FFN_CHUNK
        fs = slice(f0, f0 + FFN_CHUNK)
        gp_tiles = gp_scr.at[n % 2]
        _to_lane_tiles(gp_tiles, jnp.dot(hf_scr[rd], wg_ref[:, fs], preferred_element_type=jnp.float32))
        v = jnp.dot(hm_scr[rd], wu_ref[:, fs], preferred_element_type=jnp.float32)
        for c in range(FFN_CHUNK // V7X_LANES):
            cs = slice(f0 + c * V7X_LANES, f0 + (c + 1) * V7X_LANES)
            conv = _dwconv_tile(gp_tiles, c, cfw_ref, cs, lead_f, t)
            vc = v[:, c * V7X_LANES:(c + 1) * V7X_LANES]
            act_scr[:, cs] = (conv * jax.nn.sigmoid(conv) * vc).astype(jnp.bfloat16)

    uc = c_scr[rd]
    mu = jnp.mean(uc, axis=-1, keepdims=True)
    cen = uc - mu
    var = jnp.mean(cen * cen, axis=-1, keepdims=True)
    ln = cen * lax.rsqrt(var + LN_EPS) * lng_ref[...] + lnb_ref[...]
    yb_scr[...] = (ln * jax.nn.sigmoid(ln)).astype(jnp.bfloat16)

    up_chunk(0)

    gm = gm_ref[...]
    xp = jnp.where(first1, 0.0, xp_ref[0])
    xn = jnp.where(last1, 0.0, xn_ref[0])
    h_scr[0:X_HALO, :] = (_rms_scale(xp) * gm).astype(jnp.bfloat16)
    h_scr[X_HALO:X_HALO + t, :] = (_rms_scale(xm_ref[0]) * gm).astype(jnp.bfloat16)
    h_scr[X_HALO + t:, :] = (_rms_scale(xn) * gm).astype(jnp.bfloat16)
    out0 = X_HALO - FFN_HALO
    x_new = x_scr.at[x1]
    x_new[0:FFN_HALO, :] = xp[out0:, :]
    x_new[FFN_HALO:FFN_HALO + t, :] = xm_ref[0]
    x_new[FFN_HALO + t:, :] = xn[0:FFN_HALO, :]

    h_b = h_scr[...]
    b_val = jnp.dot(h_b, win_ref[:, 3 * d_a:3 * d_a + d_b], preferred_element_type=jnp.float32)
    b_gate = jnp.dot(h_b, win_ref[:, 3 * d_a + d_b:], preferred_element_type=jnp.float32)
    _to_lane_tiles(u_scr, b_val * jax.nn.sigmoid(b_gate))

    up_chunk(1)

    h_a = h_scr[A_TRIM:t + 2 * X_HALO - A_TRIM, :]
    a_h = jnp.dot(h_a, win_ref[:, 0:d_a], preferred_element_type=jnp.float32)
    a_cg = jnp.dot(h_a, win_ref[:, 2 * d_a:3 * d_a], preferred_element_type=jnp.float32)
    _to_lane_tiles(p_scr, a_cg * a_h)
    a_bg = jnp.dot(h_a, win_ref[:, d_a:2 * d_a], preferred_element_type=jnp.float32)
    lead_a = out0 - A_TRIM - caw_ref.shape[0] // 2
    y_a = []
    for c in range(d_a // V7X_LANES):
        cs = slice(c * V7X_LANES, (c + 1) * V7X_LANES)
        conv_a = _dwconv_tile(p_scr, c, caw_ref, cs, lead_a, tm)
        y_a.append((a_bg[out0 - A_TRIM:out0 - A_TRIM + tm, cs] * conv_a).astype(jnp.bfloat16))
    ya_scr[wr] = jnp.concatenate(y_a, axis=1)

    lead_b = out0 - cbw_ref.shape[0] // 2
    c_new = c_scr.at[wr]
    pieces = [(c, r0) for c in range(d_b // V7X_LANES) for r0 in range(0, tm, CONV_ROWS)]
    def finish_mixer():
        mix = (jnp.dot(ya_scr[rd], wout_ref[0:d_a, :], preferred_element_type=jnp.float32)
               + jnp.dot(yb_scr[...], wout_ref[d_a:, :], preferred_element_type=jnp.float32))
        xmid = x_scr.at[x2]
        xmid[0:FFN_HALO, :] = jnp.where(first2, 0.0, xmid[0:FFN_HALO, :] + mix[0:FFN_HALO])
        xmid[FFN_HALO:FFN_HALO + t, :] = xmid[FFN_HALO:FFN_HALO + t, :] + mix[FFN_HALO:FFN_HALO + t]
        xmid[FFN_HALO + t:, :] = jnp.where(last2, 0.0, xmid[FFN_HALO + t:, :] + mix[FFN_HALO + t:])
        hn = _rms_scale(xmid[...]) * gf_ref[...]
        hf_scr[wr] = hn.astype(jnp.bfloat16)
        hm_scr[wr] = hn[FFN_HALO:FFN_HALO + t].astype(jnp.bfloat16)

    for n in range(2, n_chunks):
        up_chunk(n)
        k = CONV_PIECES_PER_CHUNK * (n - 2)
        for c, r0 in pieces[k:k + CONV_PIECES_PER_CHUNK]:
            cs = slice(c * V7X_LANES, (c + 1) * V7X_LANES)
            conv_b = _dwconv_tile(u_scr, c, cbw_ref, cs, r0 + lead_b, CONV_ROWS)
            c_new[r0:r0 + CONV_ROWS, cs] = conv_b + cbb_ref[:, cs]
        if n == n_chunks - W_OUT_BEFORE_LAST_CHUNKS - 1:
            finish_mixer()

    half = t // 2
    for r0 in (0, half):
        y = jnp.dot(act_scr[r0:r0 + half, :], wd_ref[...], preferred_element_type=jnp.float32)
        x_res = x_scr[x3, FFN_HALO + r0:FFN_HALO + r0 + half, :]
        o_ref[0, r0:r0 + half, :] = _rms_scale(x_res + y) * gz_ref[...]


def _const_spec(shape):
    return pl.BlockSpec(shape, lambda s: (0,) * len(shape), pipeline_mode=pl.Buffered(1))


def _block(x, gm, w_in, conv_a_w, conv_b_w, conv_b_b, ln_g, ln_b, w_out,
           gf, w_gate, w_up, conv_f_w, w_down, gz):
    bsz, seq, d = x.shape
    t = SEQ_TILE
    d_a, d_b, d_ff = conv_a_w.shape[1], conv_b_w.shape[1], w_gate.shape[1]
    tm = t + 2 * FFN_HALO
    n_chunks = d_ff // FFN_CHUNK
    assert seq % t == 0 and t % X_HALO == 0 and tm % CONV_ROWS == 0 and d_ff % FFN_CHUNK == 0
    assert (d_b // V7X_LANES) * (tm // CONV_ROWS) <= CONV_PIECES_PER_CHUNK * (n_chunks - 2)
    assert conv_b_w.shape[0] // 2 + FFN_HALO <= X_HALO and conv_a_w.shape[0] // 2 + FFN_HALO <= X_HALO - A_TRIM
    tiles_per_seq = seq // t
    n_tiles = bsz * tiles_per_seq
    halo_per_tile = t // X_HALO
    last_halo_block = seq // X_HALO - 1

    def in_tile(s):
        tile = jnp.minimum(s, n_tiles - 1)
        return tile // tiles_per_seq, tile % tiles_per_seq

    def prev_map(s):
        b, i = in_tile(s)
        return b, jnp.maximum(i * halo_per_tile - 1, 0), 0

    def main_map(s):
        b, i = in_tile(s)
        return b, i, 0

    def next_map(s):
        b, i = in_tile(s)
        return b, jnp.minimum((i + 1) * halo_per_tile, last_halo_block), 0

    def out_map(s):
        tile = jnp.maximum(s - (PIPELINE_STAGES - 1), 0)
        return tile // tiles_per_seq, tile % tiles_per_seq, 0

    params = [gm, w_in, conv_a_w, conv_b_w, conv_b_b, ln_g, ln_b, w_out, gf, w_gate, w_up, conv_f_w, w_down, gz]
    f32, bf16 = jnp.float32, jnp.bfloat16
    return pl.pallas_call(
        functools.partial(_block_kernel, tiles_per_seq),
        grid=(n_tiles + PIPELINE_STAGES - 1,),
        in_specs=[pl.BlockSpec((1, X_HALO, d), prev_map), pl.BlockSpec((1, t, d), main_map),
                  pl.BlockSpec((1, X_HALO, d), next_map)] + [_const_spec(p.shape) for p in params],
        out_specs=pl.BlockSpec((1, t, d), out_map),
        out_shape=jax.ShapeDtypeStruct(x.shape, x.dtype),
        scratch_shapes=[
            pltpu.VMEM((PIPELINE_STAGES, tm, d), f32),
            pltpu.VMEM((t + 2 * X_HALO, d), bf16),
            pltpu.VMEM((2, tm, d), bf16),
            pltpu.VMEM((2, t, d), bf16),
            pltpu.VMEM((d_a // V7X_LANES, t + 2 * (X_HALO - A_TRIM), V7X_LANES), f32),
            pltpu.VMEM((d_b // V7X_LANES, t + 2 * X_HALO, V7X_LANES), f32),
            pltpu.VMEM((2, tm, d_b), f32),
            pltpu.VMEM((2, tm, d_a), bf16),
            pltpu.VMEM((tm, d_b), bf16),
            pltpu.VMEM((2, FFN_CHUNK // V7X_LANES, tm, V7X_LANES), f32),
            pltpu.VMEM((t, d_ff), bf16),
        ],
        compiler_params=pltpu.CompilerParams(
            dimension_semantics=("arbitrary",), vmem_limit_bytes=VMEM_LIMIT_BYTES),
        name="block",
    )(x, x, x, *params)


def kernel(x, norm_mix_g, w_in, conv_a_w, conv_b_w, conv_b_b, ln_b_g, ln_b_b, w_out,
           norm_ffn_g, w_gate, w_up, conv_ffn_w, w_down, norm_final_g):
    depth = w_in.shape[0]
    assert depth == 1, "the final norm is fused into the last layer's call"
    bf16 = jnp.bfloat16
    row = lambda v: v.reshape(1, -1)
    return _block(x, row(norm_mix_g[0]), w_in[0].astype(bf16), conv_a_w[0], conv_b_w[0],
                  row(conv_b_b[0]), row(ln_b_g[0]), row(ln_b_b[0]), w_out[0].astype(bf16),
                  row(norm_ffn_g[0]), w_gate[0].astype(bf16), w_up[0].astype(bf16),
                  conv_ffn_w[0], w_down[0].astype(bf16), row(norm_final_g))
```
